```python
import math
import jax
import jax.numpy as jnp
from jax import lax
import numpy as np


D_MODEL = 1024
BATCH = 8
SEQ = 2048
DEPTH = 4

HEAD_DIM = 128
MIX = D_MODEL // 2
GM_CHUNK = 128
GM_HEADS = MIX // HEAD_DIM
S5_GROUP = 16
S5_GROUPS = MIX // S5_GROUP
S5_STATE = 64
S5_DT_MIN = 1e-3
S5_DT_MAX = 1e-1
ML_HEADS = MIX // HEAD_DIM
ML_CHUNK = 128
CONV_W = 4
MB_HEADS = MIX // HEAD_DIM
MB_BLOCK = 256
MB_TOPK = 3
MB_QBLOCK = 16
ROPE_THETA = 10000.0
N_MEM = 256
XA_HEADS = 4
XA_HEAD_DIM = D_MODEL // XA_HEADS
XA_DIM = XA_HEADS * XA_HEAD_DIM
N_GROUPS = 4
EXP_PER_GROUP = 4
N_EXPERTS = N_GROUPS * EXP_PER_GROUP
EXP_FF = D_MODEL // 4
EXP_TOPK = 2
EPS = 1e-6
N_EVEN = (DEPTH + 1) // 2
N_ODD = DEPTH // 2
EVEN_IN = 3 * MIX
ODD_IN = 7 * MIX + 2 * ML_HEADS

kernel_name = 'hybrid_gmlp_s5_mlstm_moba_hmoe'


def rmsnorm(x, g):
    xf = x.astype(jnp.float32)
    y = xf * lax.rsqrt(jnp.mean(xf * xf, axis=-1, keepdims=True) + EPS)
    return (y * g.astype(jnp.float32)).astype(x.dtype)


def layernorm(x, g):
    xf = x.astype(jnp.float32)
    xc = xf - jnp.mean(xf, axis=-1, keepdims=True)
    y = xc * lax.rsqrt(jnp.mean(xc * xc, axis=-1, keepdims=True) + EPS)
    return (y * g.astype(jnp.float32)).astype(x.dtype)


def split_heads(t, n_heads):
    b, s, w = t.shape
    return t.reshape(b, s, n_heads, w // n_heads).transpose(0, 2, 1, 3)


def merge_heads(t):
    b, h, s, d = t.shape
    return t.transpose(0, 2, 1, 3).reshape(b, s, h * d)


def rope(x):
    s, d = x.shape[-2], x.shape[-1]
    half = d // 2
    inv_freq = ROPE_THETA ** (-jnp.arange(half, dtype=jnp.float32) / half)
    ang = jnp.arange(s, dtype=jnp.float32)[:, None] * inv_freq[None, :]
    cos, sin = jnp.cos(ang), jnp.sin(ang)
    xf = x.astype(jnp.float32)
    x1, x2 = xf[..., :half], xf[..., half:]
    return jnp.concatenate([x1 * cos - x2 * sin, x1 * sin + x2 * cos], axis=-1).astype(x.dtype)


def gmlp_mixer(u, v, ws, bs, ln_g):
    b, s, _ = u.shape
    nc = s // GM_CHUNK
    vh = layernorm(v.reshape(b, s, GM_HEADS, HEAD_DIM), ln_g.reshape(GM_HEADS, HEAD_DIM))
    vh = vh.reshape(b, nc, GM_CHUNK, GM_HEADS, HEAD_DIM)
    causal = jnp.tril(jnp.ones((GM_CHUNK, GM_CHUNK), dtype=bool))
    w = jnp.where(causal[None], ws, jnp.zeros_like(ws))
    mixed = jnp.einsum('hts,bcshd->bcthd', w, vh) + bs.T[None, None, :, :, None]
    return u * mixed.reshape(b, s, MIX).astype(u.dtype)


def _linear_combine(e1, e2):
    a1, b1 = e1
    a2, b2 = e2
    return a1 * a2, a2 * b1 + b2


def s5_mixer(u, lam_re, lam_im, b_re, b_im, c_re, c_im, d_skip, log_step, w_glu, b_glu):
    f32 = jnp.float32
    bsz, s, _ = u.shape
    uf = u.astype(f32).reshape(bsz, s, S5_GROUPS, S5_GROUP)
    lam = lax.complex(lam_re.astype(f32), lam_im.astype(f32))
    step = jnp.exp(log_step.astype(f32))[:, None]
    lam_bar = jnp.exp(lam * step)
    b_mat = lax.complex(b_re.astype(f32), b_im.astype(f32))
    b_bar = ((lam_bar - 1.0) / lam)[..., None] * b_mat
    bu = jnp.einsum('bsgi,gpi->bsgp', uf.astype(jnp.complex64), b_bar)
    a = jnp.broadcast_to(lam_bar, bu.shape)
    _, states = lax.associative_scan(_linear_combine, (a, bu), axis=1)
    c_mat = lax.complex(c_re.astype(f32), c_im.astype(f32))
    y = jnp.einsum('bsgp,gip->bsgi', states, c_mat).real
    y = y + d_skip.astype(f32).reshape(S5_GROUPS, S5_GROUP) * uf
    y = jax.nn.gelu(y.reshape(bsz, s, MIX))
    return (y * jax.nn.sigmoid(y @ w_glu.astype(f32) + b_glu.astype(f32))).astype(u.dtype)


def causal_conv(x, w, b):
    c = x.shape[-1]
    y = lax.conv_general_dilated(x, w[:, None, :].astype(x.dtype), window_strides=(1,),
                                 padding=[(CONV_W - 1, 0)],
                                 dimension_numbers=('NWC', 'WIO', 'NWC'),
                                 feature_group_count=c)
    return y + b.astype(x.dtype)


def mlstm_chunkwise(q, k, v, ig, fg):
    bsz, h, s, d = q.shape
    L = ML_CHUNK
    nc = s // L
    q = q.reshape(bsz, h, nc, L, d)
    k = k.reshape(bsz, h, nc, L, d)
    v = v.reshape(bsz, h, nc, L, d)
    ii = ig.reshape(bsz, h, nc, L)
    bcum = jnp.cumsum(jax.nn.log_sigmoid(fg).reshape(bsz, h, nc, L), axis=-1)
    g = bcum[..., -1]
    a = g[..., None] - bcum + ii

    def step(carry, xs):
        c_st, n_st, m_st = carry
        kc, vc, ac, gc = xs
        m_new = jnp.maximum(gc + m_st, jnp.max(ac, axis=-1))
        w = jnp.exp(ac - m_new[..., None])
        decay = jnp.exp(gc + m_st - m_new)
        kw = kc * w[..., None]
        c_new = decay[..., None, None] * c_st + jnp.einsum('bhld,bhle->bhde', kw, vc)
        n_new = decay[..., None] * n_st + jnp.sum(kw, axis=2)
        return (c_new, n_new, m_new), (c_st, n_st, m_st)

    init = (jnp.zeros((bsz, h, d, d), jnp.float32), jnp.zeros((bsz, h, d), jnp.float32),
            jnp.zeros((bsz, h), jnp.float32))
    xs = (jnp.moveaxis(k, 2, 0), jnp.moveaxis(v, 2, 0), jnp.moveaxis(a, 2, 0), jnp.moveaxis(g, 2, 0))
    _, (cs, ns, ms) = lax.scan(step, init, xs)
    cs = jnp.moveaxis(cs, 0, 2)
    ns = jnp.moveaxis(ns, 0, 2)
    ms = jnp.moveaxis(ms, 0, 2)

    causal = jnp.tril(jnp.ones((L, L), dtype=bool))
    dlog = jnp.where(causal, bcum[..., :, None] - bcum[..., None, :] + ii[..., None, :], -jnp.inf)
    inter_log = bcum + ms[..., None]
    m_t = jnp.maximum(inter_log, jnp.max(dlog, axis=-1))
    dw = jnp.exp(dlog - m_t[..., None])
    inter_w = jnp.exp(inter_log - m_t)
    sc = jnp.einsum('bhctd,bhcsd->bhcts', q, k) * dw
    num = inter_w[..., None] * jnp.einsum('bhctd,bhcde->bhcte', q, cs) + jnp.einsum('bhcts,bhcse->bhcte', sc, v)
    den = inter_w * jnp.einsum('bhctd,bhcd->bhct', q, ns) + jnp.sum(sc, axis=-1)
    out = num / jnp.maximum(jnp.abs(den), jnp.exp(-m_t))[..., None]
    return out.reshape(bsz, h, s, d)


def mlstm_mixer(c_qk, c_v, c_o, c_if, conv_w, conv_b, hn_g):
    f32 = jnp.float32
    qk = jax.nn.silu(causal_conv(c_qk, conv_w, conv_b))
    q = split_heads(qk[..., :MIX], ML_HEADS).astype(f32)
    k = split_heads(qk[..., MIX:], ML_HEADS).astype(f32) * (HEAD_DIM ** -0.5)
    v = split_heads(c_v, ML_HEADS).astype(f32)
    ig = c_if[..., :ML_HEADS].astype(f32).transpose(0, 2, 1)
    fg = c_if[..., ML_HEADS:].astype(f32).transpose(0, 2, 1)
    hh = mlstm_chunkwise(q, k, v, ig, fg)
    hh = rmsnorm(hh, hn_g.reshape(ML_HEADS, 1, HEAD_DIM))
    return (merge_heads(hh) * jax.nn.sigmoid(c_o.astype(f32))).astype(c_v.dtype)


_gather_blocks = jax.vmap(jax.vmap(lambda blocks, ids: blocks[ids]))


def moba_attention(q, k, v):
    bsz, h, s, d = q.shape
    nb = -(-s // MB_BLOCK)
    pad = nb * MB_BLOCK - s
    kb = jnp.pad(k, ((0, 0), (0, 0), (0, pad), (0, 0))).reshape(bsz, h, nb, MB_BLOCK, d)
    vb = jnp.pad(v, ((0, 0), (0, 0), (0, pad), (0, 0))).reshape(bsz, h, nb, MB_BLOCK, d)
    kmean = jnp.mean(kb.astype(jnp.float32), axis=3)
    n_sel = min(MB_TOPK, nb - 1)
    scale = HEAD_DIM ** -0.5

    def q_block(i):
        start = i * MB_QBLOCK
        qb = lax.dynamic_slice_in_dim(q, start, MB_QBLOCK, axis=2).astype(jnp.float32)
        qpos = start + jnp.arange(MB_QBLOCK)
        own = start // MB_BLOCK
        k_own = lax.dynamic_index_in_dim(kb, own, axis=2, keepdims=False)
        v_own = lax.dynamic_index_in_dim(vb, own, axis=2, keepdims=False)
        kpos = own * MB_BLOCK + jnp.arange(MB_BLOCK)
        s_own = jnp.einsum('bhqd,bhkd->bhqk', qb, k_own) * scale
        s_own = jnp.where(kpos[None, :] <= qpos[:, None], s_own, -jnp.inf)
        if n_sel == 0:
            p = jax.nn.softmax(s_own, axis=-1)
            return jnp.einsum('bhqk,bhkd->bhqd', p, v_own)
        gate = jnp.einsum('bhqd,bhnd->bhqn', qb, kmean)
        gate = jnp.where(jnp.arange(nb) < own, gate, -jnp.inf)
        _, idx = lax.top_k(gate, n_sel)
        valid = idx < own
        k_sel = _gather_blocks(kb, idx)
        v_sel = _gather_blocks(vb, idx)
        s_sel = jnp.einsum('bhqd,bhqnkd->bhqnk', qb, k_sel) * scale
        s_sel = jnp.where(valid[..., None], s_sel, -jnp.inf)
        s_all = jnp.concatenate([s_own, s_sel.reshape(bsz, h, MB_QBLOCK, n_sel * MB_BLOCK)], axis=-1)
        p = jax.nn.softmax(s_all, axis=-1)
        p_own = p[..., :MB_BLOCK]
        p_sel = p[..., MB_BLOCK:].reshape(bsz, h, MB_QBLOCK, n_sel, MB_BLOCK)
        return (jnp.einsum('bhqk,bhkd->bhqd', p_own, v_own)
                + jnp.einsum('bhqnk,bhqnkd->bhqd', p_sel, v_sel))

    outs = lax.map(q_block, jnp.arange(s // MB_QBLOCK))
    return outs.transpose(1, 2, 0, 3, 4).reshape(bsz, h, s, d)


def moba_mixer(d_qkv, qn_g, kn_g):
    q = rope(rmsnorm(split_heads(d_qkv[..., :MIX], MB_HEADS), qn_g))
    k = rope(rmsnorm(split_heads(d_qkv[..., MIX:2 * MIX], MB_HEADS), kn_g))
    v = split_heads(d_qkv[..., 2 * MIX:], MB_HEADS)
    return merge_heads(moba_attention(q, k, v)).astype(d_qkv.dtype)


def even_mixer(xn, w_in, w_out, gm_ws, gm_bs, gm_ln_g, lam_re, lam_im, b_re, b_im, c_re, c_im,
               d_skip, log_step, w_glu, b_glu):
    z = xn @ w_in
    uv = jax.nn.gelu(z[..., :2 * MIX])
    ya = gmlp_mixer(uv[..., :MIX], uv[..., MIX:], gm_ws, gm_bs, gm_ln_g)
    yb = s5_mixer(z[..., 2 * MIX:], lam_re, lam_im, b_re, b_im, c_re, c_im, d_skip, log_step, w_glu, b_glu)
    return (jnp.concatenate([ya, yb], axis=-1) @ w_out).astype(xn.dtype)


def odd_mixer(xn, w_in, w_out, conv_w, conv_b, b_if, hn_g, qn_g, kn_g):
    z = xn @ w_in
    c_qk = z[..., :2 * MIX]
    c_v = z[..., 2 * MIX:3 * MIX]
    c_o = z[..., 3 * MIX:4 * MIX]
    c_if = z[..., 4 * MIX:4 * MIX + 2 * ML_HEADS] + b_if
    d_qkv = z[..., 4 * MIX + 2 * ML_HEADS:]
    yc = mlstm_mixer(c_qk, c_v, c_o, c_if, conv_w, conv_b, hn_g)
    yd = moba_mixer(d_qkv, qn_g, kn_g)
    return (jnp.concatenate([yc, yd], axis=-1) @ w_out).astype(xn.dtype)


def cross_attn(xn, mem_k, mem_v, wq, wo, qn_g, kn_g):
    q = rmsnorm(split_heads(xn @ wq, XA_HEADS), qn_g)
    k = rmsnorm(mem_k, kn_g)
    s = jnp.einsum('bhsd,bhmd->bhsm', q, k).astype(jnp.float32) * (XA_HEAD_DIM ** -0.5)
    p = jax.nn.softmax(s, axis=-1).astype(xn.dtype)
    o = jnp.einsum('bhsm,bhmd->bhsd', p, mem_v)
    return (merge_heads(o) @ wo).astype(xn.dtype)


def hier_moe(xn, w_grp, b_grp, w_exp, b_exp, w1, w3, w2):
    bsz, s, dm = xn.shape
    t = xn.reshape(-1, dm)
    grp_logits = (t @ w_grp + b_grp).astype(jnp.float32)
    grp_p = jax.nn.softmax(grp_logits, axis=-1)
    grp = jnp.argmax(grp_logits, axis=-1)
    exp_logits = (t @ w_exp + b_exp).astype(jnp.float32).reshape(-1, N_GROUPS, EXP_PER_GROUP)
    sel = jnp.take_along_axis(exp_logits, grp[:, None, None], axis=1)[:, 0]
    top_v, top_i = lax.top_k(sel, EXP_TOPK)
    w_tok = jax.nn.softmax(top_v, axis=-1) * jnp.take_along_axis(grp_p, grp[:, None], axis=1)
    eid = grp[:, None] * EXP_PER_GROUP + top_i
    gate = jnp.sum(jax.nn.one_hot(eid, N_EXPERTS, dtype=jnp.float32) * w_tok[..., None], axis=1)
    hid = jax.nn.silu(jnp.einsum('td,edf->tef', t, w1)) * jnp.einsum('td,edf->tef', t, w3)
    hid = hid * gate[..., None].astype(hid.dtype)
    y = hid.reshape(t.shape[0], N_EXPERTS * EXP_FF) @ w2.reshape(N_EXPERTS * EXP_FF, dm)
    return y.reshape(bsz, s, dm).astype(xn.dtype)


def setup_inputs(seed: int = 0) -> dict:
    key = jax.random.key(seed)
    it = iter(jax.random.split(key, 48))
    f32 = jnp.float32

    def nrm(shape, fan_in):
        return jax.random.normal(next(it), shape, f32) * (fan_in ** -0.5)

    def gain(shape):
        return 1.0 + 0.05 * jax.random.normal(next(it), shape, f32)

    def small(shape, sc=0.01):
        return sc * jax.random.normal(next(it), shape, f32)

    L, NE, NO = DEPTH, N_EVEN, N_ODD
    x = jax.random.normal(next(it), (BATCH, SEQ, D_MODEL), f32)
    mem = jax.random.normal(next(it), (BATCH, N_MEM, D_MODEL), f32)
    mem_norm_g = gain((D_MODEL,))
    w_mem_kv = nrm((D_MODEL, 2 * XA_DIM), D_MODEL)
    norm_mix_g = gain((L, D_MODEL))
    norm_xa_g = gain((L, D_MODEL))
    norm_ffn_g = gain((L, D_MODEL))
    xa_wq = nrm((L, D_MODEL, XA_DIM), D_MODEL)
    xa_wo = nrm((L, XA_DIM, D_MODEL), XA_DIM)
    xa_qn_g = gain((L, XA_HEAD_DIM))
    xa_kn_g = gain((L, XA_HEAD_DIM))
    moe_w_grp = nrm((L, D_MODEL, N_GROUPS), D_MODEL)
    moe_b_grp = small((L, N_GROUPS))
    moe_w_exp = nrm((L, D_MODEL, N_EXPERTS), D_MODEL)
    moe_b_exp = small((L, N_EXPERTS))
    moe_w1 = nrm((L, N_EXPERTS, D_MODEL, EXP_FF), D_MODEL)
    moe_w3 = nrm((L, N_EXPERTS, D_MODEL, EXP_FF), D_MODEL)
    moe_w2 = nrm((L, N_EXPERTS, EXP_FF, D_MODEL), EXP_FF)
    ev_w_in = nrm((NE, D_MODEL, EVEN_IN), D_MODEL)
    ev_w_out = nrm((NE, 2 * MIX, D_MODEL), 2 * MIX)
    gm_ws = nrm((NE, GM_HEADS, GM_CHUNK, GM_CHUNK), GM_CHUNK)
    gm_bs = 1.0 + small((NE, GM_HEADS, GM_CHUNK), 0.05)
    gm_ln_g = gain((NE, MIX))
    s5_lam_re = -0.5 + small((NE, S5_GROUPS, S5_STATE))
    s5_lam_im = (jnp.pi * jnp.arange(S5_STATE, dtype=f32))[None, None, :] + small((NE, S5_GROUPS, S5_STATE))
    s5_b_re = nrm((NE, S5_GROUPS, S5_STATE, S5_GROUP), 2 * S5_GROUP)
    s5_b_im = nrm((NE, S5_GROUPS, S5_STATE, S5_GROUP), 2 * S5_GROUP)
    s5_c_re = nrm((NE, S5_GROUPS, S5_GROUP, S5_STATE), 2 * S5_STATE)
    s5_c_im = nrm((NE, S5_GROUPS, S5_GROUP, S5_STATE), 2 * S5_STATE)
    s5_d = jax.random.normal(next(it), (NE, MIX), f32)
    s5_log_step = jax.random.uniform(next(it), (NE, S5_GROUPS), f32,
                                     minval=math.log(S5_DT_MIN), maxval=math.log(S5_DT_MAX))
    s5_w_glu = nrm((NE, MIX, MIX), MIX)
    s5_b_glu = small((NE, MIX))
    od_w_in = nrm((NO, D_MODEL, ODD_IN), D_MODEL)
    od_w_out = nrm((NO, 2 * MIX, D_MODEL), 2 * MIX)
    ml_conv_w = nrm((NO, CONV_W, 2 * MIX), CONV_W)
    ml_conv_b = small((NO, 2 * MIX))
    ml_b_if = jnp.concatenate([small((NO, ML_HEADS), 0.1),
                               jnp.linspace(3.0, 6.0, ML_HEADS, dtype=f32)[None, :] + small((NO, ML_HEADS))], axis=-1)
    ml_hn_g = gain((NO, MIX))
    mb_qn_g = gain((NO, HEAD_DIM))
    mb_kn_g = gain((NO, HEAD_DIM))
    return {'x': x, 'mem': mem, 'mem_norm_g': mem_norm_g, 'w_mem_kv': w_mem_kv,
            'norm_mix_g': norm_mix_g, 'norm_xa_g': norm_xa_g, 'norm_ffn_g': norm_ffn_g,
            'xa_wq': xa_wq, 'xa_wo': xa_wo, 'xa_qn_g': xa_qn_g, 'xa_kn_g': xa_kn_g,
            'moe_w_grp': moe_w_grp, 'moe_b_grp': moe_b_grp, 'moe_w_exp': moe_w_exp, 'moe_b_exp': moe_b_exp,
            'moe_w1': moe_w1, 'moe_w3': moe_w3, 'moe_w2': moe_w2,
            'ev_w_in': ev_w_in, 'ev_w_out': ev_w_out, 'gm_ws': gm_ws, 'gm_bs': gm_bs, 'gm_ln_g': gm_ln_g,
            's5_lam_re': s5_lam_re, 's5_lam_im': s5_lam_im, 's5_b_re': s5_b_re, 's5_b_im': s5_b_im,
            's5_c_re': s5_c_re, 's5_c_im': s5_c_im, 's5_d': s5_d, 's5_log_step': s5_log_step,
            's5_w_glu': s5_w_glu, 's5_b_glu': s5_b_glu,
            'od_w_in': od_w_in, 'od_w_out': od_w_out, 'ml_conv_w': ml_conv_w, 'ml_conv_b': ml_conv_b,
            'ml_b_if': ml_b_if, 'ml_hn_g': ml_hn_g, 'mb_qn_g': mb_qn_g, 'mb_kn_g': mb_kn_g}


def reference(x, mem, mem_norm_g, w_mem_kv, norm_mix_g, norm_xa_g, norm_ffn_g, xa_wq, xa_wo, xa_qn_g,
              xa_kn_g, moe_w_grp, moe_b_grp, moe_w_exp, moe_b_exp, moe_w1, moe_w3, moe_w2, ev_w_in,
              ev_w_out, gm_ws, gm_bs, gm_ln_g, s5_lam_re, s5_lam_im, s5_b_re, s5_b_im, s5_c_re, s5_c_im,
              s5_d, s5_log_step, s5_w_glu, s5_b_glu, od_w_in, od_w_out, ml_conv_w, ml_conv_b, ml_b_if,
              ml_hn_g, mb_qn_g, mb_kn_g):
    mkv = rmsnorm(mem, mem_norm_g) @ w_mem_kv
    mem_k = split_heads(mkv[..., :XA_DIM], XA_HEADS)
    mem_v = split_heads(mkv[..., XA_DIM:], XA_HEADS)
    h = x
    for layer in range(DEPTH):
        j = layer // 2
        xn = rmsnorm(h, norm_mix_g[layer])
        if layer % 2 == 0:
            mix = even_mixer(xn, ev_w_in[j], ev_w_out[j], gm_ws[j], gm_bs[j], gm_ln_g[j],
                             s5_lam_re[j], s5_lam_im[j], s5_b_re[j], s5_b_im[j], s5_c_re[j], s5_c_im[j],
                             s5_d[j], s5_log_step[j], s5_w_glu[j], s5_b_glu[j])
        else:
            mix = odd_mixer(xn, od_w_in[j], od_w_out[j], ml_conv_w[j], ml_conv_b[j], ml_b_if[j],
                            ml_hn_g[j], mb_qn_g[j], mb_kn_g[j])
        h = h + mix
        h = h + cross_attn(rmsnorm(h, norm_xa_g[layer]), mem_k, mem_v, xa_wq[layer], xa_wo[layer],
                           xa_qn_g[layer], xa_kn_g[layer])
        h = h + hier_moe(rmsnorm(h, norm_ffn_g[layer]), moe_w_grp[layer], moe_b_grp[layer],
                         moe_w_exp[layer], moe_b_exp[layer], moe_w1[layer], moe_w3[layer], moe_w2[layer])
    return h
```

```python
import functools
import math

import jax
import jax.numpy as jnp
from jax import lax
from jax.experimental import pallas as pl
from jax.experimental.pallas import tpu as pltpu

F32 = jnp.float32
BF16 = jnp.bfloat16

D_MODEL = 1024
HEAD_DIM = 128
MIX = D_MODEL // 2
GM_CHUNK = 128
GM_HEADS = MIX // HEAD_DIM
S5_GROUP = 16
S5_GROUPS = MIX // S5_GROUP
S5_STATE = 64
S5_CHUNK = 64
ML_HEADS = MIX // HEAD_DIM
ML_CHUNK = 128
CONV_W = 4
MB_HEADS = MIX // HEAD_DIM
MB_BLOCK = 256
MB_TOPK = 3
ROPE_THETA = 10000.0
XA_HEADS = 4
XA_HEAD_DIM = D_MODEL // XA_HEADS
N_GROUPS = 4
EXP_PER_GROUP = 4
N_EXPERTS = N_GROUPS * EXP_PER_GROUP
EXP_FF = D_MODEL // 4
EPS = 1e-6
LANES = 128
VMEM_LIMIT = 56 * 1024 * 1024

NEG_INF = float("-inf")


def _cparams(sem):
    return pltpu.CompilerParams(dimension_semantics=sem, vmem_limit_bytes=VMEM_LIMIT)


def _rms(x, g):
    return x * lax.rsqrt(jnp.mean(x * x, axis=-1, keepdims=True) + EPS) * g


def _bdot(a, b):
    return jnp.dot(a.astype(BF16), b.astype(BF16), preferred_element_type=F32)


def _bdot_nt(a, b):
    return lax.dot_general(a.astype(BF16), b.astype(BF16), (((1,), (1,)), ((), ())),
                           preferred_element_type=F32)


def _bdot_tn(a, b):
    return lax.dot_general(a.astype(BF16), b.astype(BF16), (((0,), (0,)), ((), ())),
                           preferred_element_type=F32)


def _norm_matmul_body(x_ref, g_ref, w_ref, o_ref):
    xn = _rms(x_ref[...], g_ref[...])
    o_ref[...] = _bdot(xn, w_ref[...]).astype(o_ref.dtype)


def norm_matmul(x, g, w, out_dtype, tm, name):
    t, d = x.shape
    n = w.shape[1]
    return pl.pallas_call(
        _norm_matmul_body,
        grid=(t // tm,),
        in_specs=[pl.BlockSpec((tm, d), lambda i: (i, 0)),
                  pl.BlockSpec((1, d), lambda i: (0, 0)),
                  pl.BlockSpec((d, n), lambda i: (0, 0))],
        out_specs=pl.BlockSpec((tm, n), lambda i: (i, 0)),
        out_shape=jax.ShapeDtypeStruct((t, n), out_dtype),
        compiler_params=_cparams(("parallel",)),
        name=name,
    )(x, g.reshape(1, d), w)


def _norm_matmul_gates_body(x_ref, g_ref, w_ref, wg_ref, bg_ref, o_ref, og_ref):
    xn = _rms(x_ref[...], g_ref[...]).astype(BF16)
    o_ref[...] = jnp.dot(xn, w_ref[...], preferred_element_type=F32).astype(o_ref.dtype)
    og_ref[...] = jnp.dot(xn, wg_ref[...], preferred_element_type=F32) + bg_ref[...]


def norm_matmul_gates(x, g, w, wg, bg, tm, name):
    t, d = x.shape
    n = w.shape[1]
    ng = wg.shape[1]
    return pl.pallas_call(
        _norm_matmul_gates_body,
        grid=(t // tm,),
        in_specs=[pl.BlockSpec((tm, d), lambda i: (i, 0)),
                  pl.BlockSpec((1, d), lambda i: (0, 0)),
                  pl.BlockSpec((d, n), lambda i: (0, 0)),
                  pl.BlockSpec((d, ng), lambda i: (0, 0)),
                  pl.BlockSpec((1, ng), lambda i: (0, 0))],
        out_specs=[pl.BlockSpec((tm, n), lambda i: (i, 0)),
                   pl.BlockSpec((tm, ng), lambda i: (i, 0))],
        out_shape=[jax.ShapeDtypeStruct((t, n), BF16), jax.ShapeDtypeStruct((t, ng), F32)],
        compiler_params=_cparams(("parallel",)),
        name=name,
    )(x, g.reshape(1, d), w, wg, bg)


def _s5_tables(lam_re, lam_im, b_re, b_im, c_re, c_im, log_step):
    ls = S5_CHUNK
    lam = lax.complex(lam_re.astype(F32), lam_im.astype(F32))
    step = jnp.exp(log_step.astype(F32))[:, None]
    lam_dt = lam * step
    lam_bar = jnp.exp(lam_dt)
    b_bar = ((lam_bar - 1.0) / lam)[..., None] * lax.complex(b_re.astype(F32), b_im.astype(F32))
    c_mat = lax.complex(c_re.astype(F32), c_im.astype(F32))
    tau = jnp.arange(ls + 1, dtype=F32)
    pw = jnp.exp(lam_dt[:, None, :] * tau[None, :, None])
    g = lam.shape[0]
    k = ls * S5_GROUP
    kk = jnp.einsum('gop,gtp,gpi->gtoi', c_mat, pw[:, :ls], b_bar).real
    lag = jnp.arange(ls)[None, :] - jnp.arange(ls)[:, None]
    m = jnp.where((lag >= 0)[None, :, :, None, None], kk[:, jnp.clip(lag, 0, ls - 1)], 0.0)
    m = m.transpose(0, 1, 4, 2, 3).reshape(g, k, k)
    wb = pw[:, ls - 1::-1][:, :ls, :, None] * b_bar[:, None]
    wb = wb.transpose(0, 1, 3, 2).reshape(g, k, S5_STATE)
    wb = jnp.concatenate([wb.real, wb.imag], axis=-1)
    wc = c_mat[:, None] * pw[:, 1:ls + 1][:, :, None, :]
    wc = wc.transpose(0, 3, 1, 2).reshape(g, S5_STATE, k)
    wc = jnp.concatenate([wc.real, -wc.imag], axis=1)
    a = pw[:, ls]
    a1 = jnp.concatenate([a.real, a.real], axis=-1)[:, None, :]
    a2 = jnp.concatenate([-a.imag, a.imag], axis=-1)[:, None, :]
    return m.astype(BF16), wb.astype(BF16), wc.astype(BF16), a1, a2


def _s5_body(u_ref, m_ref, wb_ref, wc_ref, a1_ref, a2_ref, y_ref, s_scr, h_scr, *, nb, nc):
    u = u_ref[0]
    s_scr[...] = jnp.dot(u, wb_ref[0], preferred_element_type=F32)
    a1 = a1_ref[0]
    a2 = a2_ref[0]
    h_scr[pl.ds(0, nb), :] = jnp.zeros((nb, 2 * S5_STATE), F32)

    def step(c, h):
        r0 = pl.multiple_of((c - 1) * nb, nb)
        h = h * a1 + pltpu.roll(h, S5_STATE, 1) * a2 + s_scr[pl.ds(r0, nb), :]
        h_scr[pl.ds(pl.multiple_of(c * nb, nb), nb), :] = h
        return h

    lax.fori_loop(1, nc, step, jnp.zeros((nb, 2 * S5_STATE), F32))
    y = jnp.dot(u, m_ref[0], preferred_element_type=F32)
    y = y + jnp.dot(h_scr[...].astype(BF16), wc_ref[0], preferred_element_type=F32)
    y_ref[0] = y.astype(y_ref.dtype)


def s5_scan(u, tables, nb, nc):
    m, wb, wc, a1, a2 = tables
    g, r, k = u.shape
    p2 = 2 * S5_STATE
    return pl.pallas_call(
        functools.partial(_s5_body, nb=nb, nc=nc),
        grid=(g,),
        in_specs=[pl.BlockSpec((1, r, k), lambda i: (i, 0, 0)),
                  pl.BlockSpec((1, k, k), lambda i: (i, 0, 0)),
                  pl.BlockSpec((1, k, p2), lambda i: (i, 0, 0)),
                  pl.BlockSpec((1, p2, k), lambda i: (i, 0, 0)),
                  pl.BlockSpec((1, 1, p2), lambda i: (i, 0, 0)),
                  pl.BlockSpec((1, 1, p2), lambda i: (i, 0, 0))],
        out_specs=pl.BlockSpec((1, r, k), lambda i: (i, 0, 0)),
        out_shape=jax.ShapeDtypeStruct((g, r, k), F32),
        scratch_shapes=[pltpu.VMEM((r, p2), F32), pltpu.VMEM((r, p2), F32)],
        compiler_params=_cparams(("parallel",)),
        name="s5_scan",
    )(u, m, wb, wc, a1, a2)


def _even_out_body(h_ref, zu_ref, zv_ref, zs_ref, ys_ref, ws_ref, bs_ref, lng_ref, d_ref,
                   wglu_ref, bglu_ref, woa_ref, wob_ref, o_ref, *, tm):
    u = jax.nn.gelu(zu_ref[...].astype(F32))
    v = jax.nn.gelu(zv_ref[...].astype(F32))
    lng = lng_ref[...]
    mixed = []
    for hh in range(GM_HEADS):
        vh = v[:, hh * HEAD_DIM:(hh + 1) * HEAD_DIM]
        vc = vh - jnp.mean(vh, axis=-1, keepdims=True)
        vn = vc * lax.rsqrt(jnp.mean(vc * vc, axis=-1, keepdims=True) + EPS)
        vn = (vn * lng[:, hh * HEAD_DIM:(hh + 1) * HEAD_DIM]).astype(BF16)
        w = ws_ref[hh]
        b = bs_ref[hh]
        rows = [jnp.dot(w, vn[c * GM_CHUNK:(c + 1) * GM_CHUNK], preferred_element_type=F32) + b
                for c in range(tm // GM_CHUNK)]
        mixed.append(jnp.concatenate(rows, axis=0))
    ya = u * jnp.concatenate(mixed, axis=1)
    y = ys_ref[...] + d_ref[...] * zs_ref[...].astype(F32)
    y = jax.nn.gelu(y)
    yb = y * jax.nn.sigmoid(_bdot(y, wglu_ref[...]) + bglu_ref[...])
    o_ref[...] = h_ref[...] + _bdot(ya, woa_ref[...]) + _bdot(yb, wob_ref[...])


def even_out(h, z, ys, ws, bs, lng, d, wglu, bglu, wo, tm):
    t, dm = h.shape
    row = lambda i: (i, 0)
    const2 = lambda i: (0, 0)
    const3 = lambda i: (0, 0, 0)
    return pl.pallas_call(
        functools.partial(_even_out_body, tm=tm),
        grid=(t // tm,),
        in_specs=[pl.BlockSpec((tm, dm), row),
                  pl.BlockSpec((tm, MIX), lambda i: (i, 0)),
                  pl.BlockSpec((tm, MIX), lambda i: (i, 1)),
                  pl.BlockSpec((tm, MIX), lambda i: (i, 2)),
                  pl.BlockSpec((tm, MIX), row),
                  pl.BlockSpec((GM_HEADS, GM_CHUNK, GM_CHUNK), const3),
                  pl.BlockSpec((GM_HEADS, GM_CHUNK, HEAD_DIM), const3),
                  pl.BlockSpec((1, MIX), const2),
                  pl.BlockSpec((1, MIX), const2),
                  pl.BlockSpec((MIX, MIX), const2),
                  pl.BlockSpec((1, MIX), const2),
                  pl.BlockSpec((MIX, dm), lambda i: (0, 0)),
                  pl.BlockSpec((MIX, dm), lambda i: (1, 0))],
        out_specs=pl.BlockSpec((tm, dm), row),
        out_shape=jax.ShapeDtypeStruct((t, dm), F32),
        compiler_params=_cparams(("parallel",)),
        name="even_out",
    )(h, z, z, z, ys, ws, bs, lng, d, wglu, bglu, wo, wo)


PAD_ROWS = 8


def _log_sigmoid(x):
    return jnp.minimum(x, 0.0) - jnp.log1p(jnp.exp(-jnp.abs(x)))


def _mlstm_body(q_ref, k_ref, v_ref, o_ref, gr_ref, gc_ref, cwq_ref, cbq_ref, cwk_ref, cbk_ref,
                hn_ref, y_ref, xp_scr, qs_scr, ks_scr, *, seq):
    L = ML_CHUNK
    nc = seq // L

    def conv_silu(x_ref, w_ref, b_ref, dst):
        xp_scr[pl.ds(0, PAD_ROWS), :] = jnp.zeros((PAD_ROWS, HEAD_DIM), F32)
        xp_scr[pl.ds(PAD_ROWS, seq), :] = x_ref[...].astype(F32)
        w = w_ref[0]
        acc = b_ref[0]
        for j in range(CONV_W):
            acc = acc + xp_scr[pl.ds(PAD_ROWS - (CONV_W - 1) + j, seq), :] * w[j:j + 1, :]
        dst[...] = acc * jax.nn.sigmoid(acc)

    conv_silu(q_ref, cwq_ref, cbq_ref, qs_scr)
    conv_silu(k_ref, cwk_ref, cbk_ref, ks_scr)

    ri = lax.broadcasted_iota(jnp.int32, (L, L), 0)
    ci = lax.broadcasted_iota(jnp.int32, (L, L), 1)
    causal = ri >= ci
    tri = causal.astype(F32)
    hn = hn_ref[0]

    def chunk(c, carry):
        c_st, n_st, m_st = carry
        r0 = pl.multiple_of(c * L, L)
        qc = qs_scr[pl.ds(r0, L), :]
        kc = ks_scr[pl.ds(r0, L), :] * (HEAD_DIM ** -0.5)
        vc = v_ref[pl.ds(r0, L), :].astype(F32)
        g_row = gr_ref[0, 0, c]
        g_col = gc_ref[0, 0, pl.ds(r0, L), :]
        ii_row = g_row[0:1, :]
        lf_row = _log_sigmoid(g_row[1:2, :])
        ii_col = g_col[:, 0:1]
        lf_col = _log_sigmoid(g_col[:, 1:2])
        bcum_col = jnp.sum(tri * lf_row, axis=1, keepdims=True)
        bcum_row = jnp.sum(jnp.where(ri <= ci, lf_col, 0.0), axis=0, keepdims=True)
        g_tot = jnp.sum(lf_row, axis=1, keepdims=True)
        a_col = g_tot - bcum_col + ii_col

        dlog = jnp.where(causal, bcum_col - bcum_row + ii_row, NEG_INF)
        inter_log = bcum_col + m_st
        m_t = jnp.maximum(inter_log, jnp.max(dlog, axis=1, keepdims=True))
        dw = jnp.exp(dlog - m_t)
        inter_w = jnp.exp(inter_log - m_t)
        sc = _bdot_nt(qc, kc) * dw
        num = inter_w * _bdot(qc, c_st) + _bdot(sc, vc)
        den = inter_w * jnp.sum(qc * n_st, axis=1, keepdims=True) + jnp.sum(sc, axis=1, keepdims=True)
        out = num / jnp.maximum(jnp.abs(den), jnp.exp(-m_t))
        hh = _rms(out, hn)
        y_ref[pl.ds(r0, L), :] = (hh * jax.nn.sigmoid(o_ref[pl.ds(r0, L), :].astype(F32))).astype(y_ref.dtype)

        m_new = jnp.maximum(g_tot + m_st, jnp.max(a_col, axis=0, keepdims=True))
        w = jnp.exp(a_col - m_new)
        decay = jnp.exp(g_tot + m_st - m_new)
        kw = kc * w
        c_new = decay * c_st + _bdot_tn(kw, vc)
        n_new = decay * n_st + jnp.sum(kw, axis=0, keepdims=True)
        return c_new, n_new, m_new

    init = (jnp.zeros((HEAD_DIM, HEAD_DIM), F32), jnp.zeros((1, HEAD_DIM), F32), jnp.zeros((1, 1), F32))
    lax.fori_loop(0, nc, chunk, init)


def mlstm(z, gates_row, gates_col, cwq, cbq, cwk, cbk, hn, nb, seq):
    nh = ML_HEADS
    nc = seq // ML_CHUNK
    blk = lambda off: pl.BlockSpec((seq, HEAD_DIM), lambda b, h, off=off: (b, off + h))
    per_head3 = lambda r: pl.BlockSpec((1, r, HEAD_DIM), lambda b, h: (h, 0, 0))
    return pl.pallas_call(
        functools.partial(_mlstm_body, seq=seq),
        grid=(nb, nh),
        in_specs=[blk(0), blk(nh), blk(2 * nh), blk(3 * nh),
                  pl.BlockSpec((1, 1, nc, 2, ML_CHUNK), lambda b, h: (b, h, 0, 0, 0)),
                  pl.BlockSpec((1, 1, seq, 2), lambda b, h: (b, h, 0, 0)),
                  per_head3(CONV_W), per_head3(1), per_head3(CONV_W), per_head3(1), per_head3(1)],
        out_specs=pl.BlockSpec((seq, HEAD_DIM), lambda b, h: (b, h)),
        out_shape=jax.ShapeDtypeStruct((nb * seq, MIX), BF16),
        scratch_shapes=[pltpu.VMEM((PAD_ROWS + seq, HEAD_DIM), F32),
                        pltpu.VMEM((seq, HEAD_DIM), F32),
                        pltpu.VMEM((seq, HEAD_DIM), F32)],
        compiler_params=_cparams(("parallel", "parallel")),
        name="mlstm",
    )(z, z, z, z, gates_row, gates_col, cwq, cbq, cwk, cbk, hn)


def _rope(x, cos, sin_signed):
    return x * cos + pltpu.roll(x, HEAD_DIM // 2, 1) * sin_signed


def _moba_body(q_ref, k_ref, v_ref, cos_ref, sin_ref, qn_ref, kn_ref, y_ref, kr_scr, km_scr, *, seq):
    i = pl.program_id(2)
    nblk = seq // MB_BLOCK
    blk = MB_BLOCK

    @pl.when(i == 0)
    def _():
        k = _rope(_rms(k_ref[...].astype(F32), kn_ref[...]), cos_ref[...], sin_ref[...])
        kr_scr[...] = k.astype(BF16)
        km_scr[...] = jnp.zeros((LANES, HEAD_DIM), F32)
        for j in range(nblk):
            km_scr[pl.ds(j, 1), :] = jnp.mean(k[j * blk:(j + 1) * blk], axis=0, keepdims=True)

    r0 = pl.multiple_of(i * blk, blk)
    q = _rope(_rms(q_ref[...].astype(F32), qn_ref[...]), cos_ref[pl.ds(r0, blk), :], sin_ref[pl.ds(r0, blk), :])
    scale = HEAD_DIM ** -0.5

    gate = lax.dot_general(q, km_scr[...], (((1,), (1,)), ((), ())),
                           precision=lax.Precision.HIGHEST, preferred_element_type=F32)
    lane = lax.broadcasted_iota(jnp.int32, (blk, LANES), 1)
    cand = lane < i

    qb = q.astype(BF16)
    ri = lax.broadcasted_iota(jnp.int32, (blk, blk), 0)
    ci = lax.broadcasted_iota(jnp.int32, (blk, blk), 1)
    s = _bdot_nt(qb, kr_scr[pl.ds(r0, blk), :]) * scale
    s = jnp.where(ci <= ri, s, NEG_INF)
    m0 = jnp.max(s, axis=1, keepdims=True)
    p = jnp.exp(s - m0)
    l0 = jnp.sum(p, axis=1, keepdims=True)
    acc0 = _bdot(p, v_ref[pl.ds(r0, blk), :])

    def other(j, carry):
        m, l, acc = carry
        gj = jnp.sum(jnp.where(lane == j, gate, 0.0), axis=1, keepdims=True)
        ahead = cand & ((gate > gj) | ((gate == gj) & (lane < j)))
        rank = jnp.sum(ahead.astype(F32), axis=1, keepdims=True)
        sel = rank < MB_TOPK
        c0 = pl.multiple_of(j * blk, blk)
        s = _bdot_nt(qb, kr_scr[pl.ds(c0, blk), :]) * scale
        s = jnp.where(sel, s, NEG_INF)
        m_new = jnp.maximum(m, jnp.max(s, axis=1, keepdims=True))
        alpha = jnp.exp(m - m_new)
        p = jnp.exp(s - m_new)
        l = alpha * l + jnp.sum(p, axis=1, keepdims=True)
        acc = alpha * acc + _bdot(p, v_ref[pl.ds(c0, blk), :])
        return m_new, l, acc

    m, l, acc = lax.fori_loop(0, i, other, (m0, l0, acc0))
    y_ref[...] = (acc / l).astype(y_ref.dtype)


def moba(z, cos, sin_signed, qn, kn, nb, seq, col0):
    nh = MB_HEADS
    nblk = seq // MB_BLOCK
    return pl.pallas_call(
        functools.partial(_moba_body, seq=seq),
        grid=(nb, nh, nblk),
        in_specs=[pl.BlockSpec((MB_BLOCK, HEAD_DIM), lambda b, h, i: (b * nblk + i, col0 + h)),
                  pl.BlockSpec((seq, HEAD_DIM), lambda b, h, i: (b, col0 + nh + h)),
                  pl.BlockSpec((seq, HEAD_DIM), lambda b, h, i: (b, col0 + 2 * nh + h)),
                  pl.BlockSpec((seq, HEAD_DIM), lambda b, h, i: (0, 0)),
                  pl.BlockSpec((seq, HEAD_DIM), lambda b, h, i: (0, 0)),
                  pl.BlockSpec((1, HEAD_DIM), lambda b, h, i: (0, 0)),
                  pl.BlockSpec((1, HEAD_DIM), lambda b, h, i: (0, 0))],
        out_specs=pl.BlockSpec((MB_BLOCK, HEAD_DIM), lambda b, h, i: (b * nblk + i, h)),
        out_shape=jax.ShapeDtypeStruct((nb * seq, MIX), BF16),
        scratch_shapes=[pltpu.VMEM((seq, HEAD_DIM), BF16), pltpu.VMEM((LANES, HEAD_DIM), F32)],
        compiler_params=_cparams(("parallel", "parallel", "arbitrary")),
        name="moba",
    )(z, z, z, cos, sin_signed, qn, kn)


def _out_proj_body(h_ref, ya_ref, yb_ref, wa_ref, wb_ref, o_ref):
    o_ref[...] = h_ref[...] + _bdot(ya_ref[...], wa_ref[...]) + _bdot(yb_ref[...], wb_ref[...])


def out_proj(h, ya, yb, wo, tm):
    t, dm = h.shape
    row = lambda i: (i, 0)
    return pl.pallas_call(
        _out_proj_body,
        grid=(t // tm,),
        in_specs=[pl.BlockSpec((tm, dm), row), pl.BlockSpec((tm, MIX), row), pl.BlockSpec((tm, MIX), row),
                  pl.BlockSpec((MIX, dm), lambda i: (0, 0)), pl.BlockSpec((MIX, dm), lambda i: (1, 0))],
        out_specs=pl.BlockSpec((tm, dm), row),
        out_shape=jax.ShapeDtypeStruct((t, dm), F32),
        compiler_params=_cparams(("parallel",)),
        name="odd_out",
    )(h, ya, yb, wo, wo)


def _xattn_body(h_ref, g_ref, wq_ref, wo_ref, qn_ref, kn_ref, mk_ref, mv_ref, o_ref, kn_scr):
    @pl.when(pl.program_id(1) == 0)
    def _():
        for hh in range(XA_HEADS):
            sl = slice(hh * XA_HEAD_DIM, (hh + 1) * XA_HEAD_DIM)
            kn_scr[:, sl] = _rms(mk_ref[:, sl].astype(F32), kn_ref[...]).astype(BF16)

    h = h_ref[...]
    xn = _rms(h, g_ref[...])
    q = _bdot(xn, wq_ref[...])
    outs = []
    for hh in range(XA_HEADS):
        sl = slice(hh * XA_HEAD_DIM, (hh + 1) * XA_HEAD_DIM)
        qh = _rms(q[:, sl], qn_ref[...])
        s = _bdot_nt(qh, kn_scr[:, sl]) * (XA_HEAD_DIM ** -0.5)
        s = s - jnp.max(s, axis=1, keepdims=True)
        p = jnp.exp(s)
        p = p / jnp.sum(p, axis=1, keepdims=True)
        outs.append(_bdot(p, mv_ref[:, sl]))
    o = jnp.concatenate(outs, axis=1)
    o_ref[...] = h + _bdot(o, wo_ref[...])


def xattn(h, g, wq, wo, qn, kn, mkv, nb, seq, n_mem, tm):
    t, dm = h.shape
    nt = seq // tm
    const = lambda b, i: (0, 0)
    return pl.pallas_call(
        _xattn_body,
        grid=(nb, nt),
        in_specs=[pl.BlockSpec((tm, dm), lambda b, i: (b * nt + i, 0)),
                  pl.BlockSpec((1, dm), const),
                  pl.BlockSpec((dm, dm), const),
                  pl.BlockSpec((dm, dm), const),
                  pl.BlockSpec((1, XA_HEAD_DIM), const),
                  pl.BlockSpec((1, XA_HEAD_DIM), const),
                  pl.BlockSpec((n_mem, dm), lambda b, i: (b, 0)),
                  pl.BlockSpec((n_mem, dm), lambda b, i: (b, 1))],
        out_specs=pl.BlockSpec((tm, dm), lambda b, i: (b * nt + i, 0)),
        out_shape=jax.ShapeDtypeStruct((t, dm), F32),
        scratch_shapes=[pltpu.VMEM((n_mem, dm), BF16)],
        compiler_params=_cparams(("parallel", "arbitrary")),
        name="xattn",
    )(h, g, wq, wo, qn, kn, mkv, mkv)


def _route(logits):
    lane = lax.broadcasted_iota(jnp.int32, logits.shape, 1)
    big = jnp.int32(LANES)
    gmask = lane < N_GROUPS
    gl = jnp.where(gmask, logits, NEG_INF)
    gmax = jnp.max(gl, axis=1, keepdims=True)
    grp = jnp.min(jnp.where(gl == gmax, lane, big), axis=1, keepdims=True)
    p_grp = 1.0 / jnp.sum(jnp.where(gmask, jnp.exp(gl - gmax), 0.0), axis=1, keepdims=True)
    lo = N_GROUPS + grp * EXP_PER_GROUP
    emask = (lane >= lo) & (lane < lo + EXP_PER_GROUP)
    el = jnp.where(emask, logits, NEG_INF)
    v0 = jnp.max(el, axis=1, keepdims=True)
    i0 = jnp.min(jnp.where(emask & (el == v0), lane, big), axis=1, keepdims=True)
    emask1 = emask & (lane != i0)
    el1 = jnp.where(emask1, logits, NEG_INF)
    v1 = jnp.max(el1, axis=1, keepdims=True)
    i1 = jnp.min(jnp.where(emask1 & (el1 == v1), lane, big), axis=1, keepdims=True)
    e1 = jnp.exp(v1 - v0)
    w0 = 1.0 / (1.0 + e1)
    w1 = e1 * w0
    return jnp.where(lane == i0, w0 * p_grp, 0.0) + jnp.where(lane == i1, w1 * p_grp, 0.0)


def _split_dot(x, w_hi, w_lo):
    x_hi = x.astype(BF16)
    x_lo = (x - x_hi.astype(F32)).astype(BF16)
    return (jnp.dot(x_hi, w_hi, preferred_element_type=F32)
            + jnp.dot(x_lo, w_hi, preferred_element_type=F32)
            + jnp.dot(x_hi, w_lo, preferred_element_type=F32))


def _moe_dense_body(h_ref, g_ref, wr_hi_ref, wr_lo_ref, br_ref, w1_ref, w3_ref, w2_ref, o_ref,
                    xn_scr, gate_scr, acc_scr):
    e = pl.program_id(1)

    @pl.when(e == 0)
    def _():
        xn = _rms(h_ref[...], g_ref[...])
        xn_scr[...] = xn.astype(BF16)
        gate_scr[...] = _route(_split_dot(xn, wr_hi_ref[...], wr_lo_ref[...]) + br_ref[...])
        acc_scr[...] = h_ref[...]

    xn = xn_scr[...]
    lane = lax.broadcasted_iota(jnp.int32, gate_scr.shape, 1)
    ge = jnp.sum(jnp.where(lane == N_GROUPS + e, gate_scr[...], 0.0), axis=1, keepdims=True)
    h1 = jnp.dot(xn, w1_ref[0], preferred_element_type=F32)
    h3 = jnp.dot(xn, w3_ref[0], preferred_element_type=F32)
    hid = (h1 * jax.nn.sigmoid(h1)) * h3 * ge
    acc_scr[...] += _bdot(hid, w2_ref[0])

    @pl.when(e == pl.num_programs(1) - 1)
    def _():
        o_ref[...] = acc_scr[...]


def moe_dense(h, g, wr_hi, wr_lo, br, w1, w3, w2, tm):
    t, dm = h.shape
    ne, _, ff = w1.shape
    const = lambda i, e: (0, 0)
    return pl.pallas_call(
        _moe_dense_body,
        grid=(t // tm, ne),
        in_specs=[pl.BlockSpec((tm, dm), lambda i, e: (i, 0)),
                  pl.BlockSpec((1, dm), const),
                  pl.BlockSpec((dm, LANES), const),
                  pl.BlockSpec((dm, LANES), const),
                  pl.BlockSpec((1, LANES), const),
                  pl.BlockSpec((1, dm, ff), lambda i, e: (e, 0, 0)),
                  pl.BlockSpec((1, dm, ff), lambda i, e: (e, 0, 0)),
                  pl.BlockSpec((1, ff, dm), lambda i, e: (e, 0, 0))],
        out_specs=pl.BlockSpec((tm, dm), lambda i, e: (i, 0)),
        out_shape=jax.ShapeDtypeStruct((t, dm), F32),
        scratch_shapes=[pltpu.VMEM((tm, dm), BF16), pltpu.VMEM((tm, LANES), F32), pltpu.VMEM((tm, dm), F32)],
        compiler_params=_cparams(("parallel", "arbitrary")),
        name="moe",
    )(h, g, wr_hi, wr_lo, br, w1, w3, w2)


def _rope_tables(seq):
    half = HEAD_DIM // 2
    inv_freq = ROPE_THETA ** (-jnp.arange(half, dtype=F32) / half)
    ang = jnp.arange(seq, dtype=F32)[:, None] * inv_freq[None, :]
    cos, sin = jnp.cos(ang), jnp.sin(ang)
    return jnp.concatenate([cos, cos], axis=1), jnp.concatenate([-sin, sin], axis=1)


def _even_layer(h, nb, seq, norm_g, w_in, w_out, gm_ws, gm_bs, gm_ln_g, lam_re, lam_im, b_re, b_im,
                c_re, c_im, d_skip, log_step, w_glu, b_glu):
    z = norm_matmul(h, norm_g, w_in.astype(BF16), BF16, 256, "even_in")
    ls = S5_CHUNK
    nc = seq // ls
    us = z[:, 2 * MIX:].reshape(nb, nc, ls, S5_GROUPS, S5_GROUP)
    us = us.transpose(3, 1, 0, 2, 4).reshape(S5_GROUPS, nc * nb, ls * S5_GROUP)
    tables = _s5_tables(lam_re, lam_im, b_re, b_im, c_re, c_im, log_step)
    ys = s5_scan(us, tables, nb, nc)
    ys = ys.reshape(S5_GROUPS, nc, nb, ls, S5_GROUP).transpose(2, 1, 3, 0, 4).reshape(nb * seq, MIX)
    causal = jnp.tril(jnp.ones((GM_CHUNK, GM_CHUNK), dtype=bool))
    ws = jnp.where(causal[None], gm_ws, 0.0).astype(BF16)
    bs = jnp.broadcast_to(gm_bs[:, :, None], (GM_HEADS, GM_CHUNK, HEAD_DIM)).astype(F32)
    return even_out(h, z, ys, ws, bs, gm_ln_g.reshape(1, MIX), d_skip.reshape(1, MIX),
                    w_glu.astype(BF16), b_glu.reshape(1, MIX), w_out.astype(BF16), 512)


def _odd_layer(h, nb, seq, norm_g, w_in, w_out, conv_w, conv_b, b_if, hn_g, qn_g, kn_g, rope_tabs):
    n_ml = 4 * MIX
    n_g = 2 * ML_HEADS
    w_main = jnp.concatenate([w_in[:, :n_ml], w_in[:, n_ml + n_g:]], axis=1).astype(BF16)
    w_gate = jnp.pad(w_in[:, n_ml:n_ml + n_g], ((0, 0), (0, LANES - n_g))).astype(BF16)
    b_gate = jnp.pad(b_if, (0, LANES - n_g)).reshape(1, LANES).astype(F32)
    z, gates = norm_matmul_gates(h, norm_g, w_main, w_gate, b_gate, 256, "odd_in")
    gts = gates[:, :n_g].reshape(nb, seq, 2, ML_HEADS)
    nc = seq // ML_CHUNK
    gates_row = gts.transpose(0, 3, 2, 1).reshape(nb, ML_HEADS, 2, nc, ML_CHUNK).transpose(0, 1, 3, 2, 4)
    gates_col = gts.transpose(0, 3, 1, 2)
    cw = conv_w.reshape(CONV_W, 2, ML_HEADS, HEAD_DIM).transpose(1, 2, 0, 3)
    cb = conv_b.reshape(2, ML_HEADS, 1, HEAD_DIM)
    yc = mlstm(z, gates_row, gates_col, cw[0], cb[0], cw[1], cb[1],
               hn_g.reshape(ML_HEADS, 1, HEAD_DIM), nb, seq)
    cos, sin_signed = rope_tabs
    yd = moba(z, cos, sin_signed, qn_g.reshape(1, HEAD_DIM), kn_g.reshape(1, HEAD_DIM), nb, seq,
              4 * ML_HEADS)
    return out_proj(h, yc, yd, w_out.astype(BF16), 512)


def kernel(x, mem, mem_norm_g, w_mem_kv, norm_mix_g, norm_xa_g, norm_ffn_g, xa_wq, xa_wo, xa_qn_g, xa_kn_g, moe_w_grp, moe_b_grp, moe_w_exp, moe_b_exp, moe_w1, moe_w3, moe_w2, ev_w_in, ev_w_out, gm_ws, gm_bs, gm_ln_g, s5_lam_re, s5_lam_im, s5_b_re, s5_b_im, s5_c_re, s5_c_im, s5_d, s5_log_step, s5_w_glu, s5_b_glu, od_w_in, od_w_out, ml_conv_w, ml_conv_b, ml_b_if, ml_hn_g, mb_qn_g, mb_kn_g):
    nb, seq, dm = x.shape
    n_mem = mem.shape[1]
    depth = norm_mix_g.shape[0]
    h = x.reshape(nb * seq, dm)
    mkv = norm_matmul(mem.reshape(nb * n_mem, dm), mem_norm_g, w_mem_kv.astype(BF16), BF16, 256, "mem_kv")
    rope_tabs = _rope_tables(seq)
    n_r = N_GROUPS + N_EXPERTS
    for layer in range(depth):
        j = layer // 2
        if layer % 2 == 0:
            h = _even_layer(h, nb, seq, norm_mix_g[layer], ev_w_in[j], ev_w_out[j], gm_ws[j], gm_bs[j],
                            gm_ln_g[j], s5_lam_re[j], s5_lam_im[j], s5_b_re[j], s5_b_im[j], s5_c_re[j],
                            s5_c_im[j], s5_d[j], s5_log_step[j], s5_w_glu[j], s5_b_glu[j])
        else:
            h = _odd_layer(h, nb, seq, norm_mix_g[layer], od_w_in[j], od_w_out[j], ml_conv_w[j],
                           ml_conv_b[j], ml_b_if[j], ml_hn_g[j], mb_qn_g[j], mb_kn_g[j], rope_tabs)
        h = xattn(h, norm_xa_g[layer].reshape(1, dm), xa_wq[layer].astype(BF16), xa_wo[layer].astype(BF16),
                  xa_qn_g[layer].reshape(1, XA_HEAD_DIM), xa_kn_g[layer].reshape(1, XA_HEAD_DIM),
                  mkv, nb, seq, n_mem, 512)
        w_r = jnp.pad(jnp.concatenate([moe_w_grp[layer], moe_w_exp[layer]], axis=1), ((0, 0), (0, LANES - n_r)))
        w_r_hi = w_r.astype(BF16)
        w_r_lo = (w_r - w_r_hi.astype(F32)).astype(BF16)
        b_r = jnp.pad(jnp.concatenate([moe_b_grp[layer], moe_b_exp[layer]]), (0, LANES - n_r)).reshape(1, LANES)
        h = moe_dense(h, norm_ffn_g[layer].reshape(1, dm), w_r_hi, w_r_lo, b_r,
                      moe_w1[layer].astype(BF16), moe_w3[layer].astype(BF16), moe_w2[layer].astype(BF16), 1024)
    return h.reshape(nb, seq, dm)
```
